```python
import math
import jax
import jax.numpy as jnp
from jax import lax
import numpy as np

D_MODEL = 2048
BATCH = 4
SEQ = 2048
DEPTH = 4
DEC_BATCH = 8
DEC_SEQ = 4
PAST_LEN = 16384
PAGE_SIZE = 128

N_MIXERS = 3
N_LAYERS_A = (DEPTH + 2) // 3
N_LAYERS_B = (DEPTH + 1) // 3
N_LAYERS_C = DEPTH // 3
RMS_EPS = 1e-6
D_FF = ((8 * D_MODEL // 3 + 255) // 256) * 256
A_DK = 256
A_HEADS = D_MODEL // A_DK
A_DV = 2 * A_DK
A_QK = A_HEADS * A_DK
A_V = A_HEADS * A_DV
RET_CHUNK = 128
ROPE_BASE = 10000.0
B_GROUPS = ((128, 1), (512, 4), (2048, 16))
N_GROUPS = len(B_GROUPS)
B_HEAD_DIM = 128
B_HEADS_PER_GROUP = D_MODEL // 512
B_WIDTH = N_GROUPS * B_HEADS_PER_GROUP * B_HEAD_DIM
NEG_INF = -1e30
C_HEAD = 64
C_HEADS = D_MODEL // C_HEAD
C_DECAY_LORA = max(32, int(round(D_MODEL ** 0.5 * 1.8 / 32)) * 32)
C_AAA_LORA = max(32, int(round(D_MODEL ** 0.5 * 1.8 / 32)) * 32)
C_GATE_LORA = max(32, int(round(D_MODEL ** 0.8 * 0.6 / 32)) * 32)
GN_EPS = 64e-5

kernel_name = 'hybrid_retention_dilated_rwkv7_macaron_step'


def rmsnorm(x, g):
    xf = x.astype(jnp.float32)
    r = lax.rsqrt(jnp.mean(xf * xf, axis=-1, keepdims=True) + RMS_EPS)
    return (xf * r).astype(x.dtype) * g


def swiglu(h, w_in, w_out):
    gate, up = jnp.split(h @ w_in, 2, axis=-1)
    return (jax.nn.silu(gate) * up) @ w_out


def rotary(x, pos):
    half = x.shape[-1] // 2
    inv = ROPE_BASE ** (-jnp.arange(half, dtype=jnp.float32) / half)
    ang = pos.astype(jnp.float32)[:, None] * inv[None, :]
    cos = jnp.cos(ang)[None, :, None, :].astype(x.dtype)
    sin = jnp.sin(ang)[None, :, None, :].astype(x.dtype)
    x1, x2 = x[..., :half], x[..., half:]
    return jnp.concatenate([x1 * cos - x2 * sin, x1 * sin + x2 * cos], axis=-1)


def retention_log_decay():
    return jnp.log1p(-jnp.exp2(-5.0 - jnp.arange(A_HEADS, dtype=jnp.float32)))


def retention_project(h, w_in, pos):
    B, T, _ = h.shape
    p = h @ w_in
    q = p[..., :A_QK].reshape(B, T, A_HEADS, A_DK)
    k = p[..., A_QK:2 * A_QK].reshape(B, T, A_HEADS, A_DK)
    v = p[..., 2 * A_QK:2 * A_QK + A_V].reshape(B, T, A_HEADS, A_DV)
    g = p[..., 2 * A_QK + A_V:]
    q = rotary(q, pos).astype(jnp.float32)
    k = (rotary(k, pos) * (A_DK ** -0.5)).astype(jnp.float32)
    return q, k, v.astype(jnp.float32), g


def retention_chunk(state, q, k, v, log_g):
    C = q.shape[1]
    i = jnp.arange(C, dtype=jnp.float32)
    dist = i[:, None] - i[None, :]
    intra = jnp.where(dist >= 0, jnp.exp(jnp.maximum(dist, 0.0)[None] * log_g[:, None, None]), 0.0)
    s = jnp.einsum('bqhd,bkhd->bhqk', q, k) * intra
    o = jnp.einsum('bhqk,bkhe->bqhe', s, v)
    q_dec = jnp.exp((i[:, None] + 1.0) * log_g[None, :])
    o = o + jnp.einsum('bqhd,bhde->bqhe', q * q_dec[None, :, :, None], state)
    k_dec = jnp.exp((C - 1.0 - i)[:, None] * log_g[None, :])
    new_state = jnp.exp(C * log_g)[None, :, None, None] * state + jnp.einsum('bkhd,bkhe->bhde', k * k_dec[None, :, :, None], v)
    return new_state, o


def retention_prompt(q, k, v, log_g):
    B, S, H, _ = q.shape
    n_chunks = S // RET_CHUNK

    def to_chunks(a):
        return a.reshape(B, n_chunks, RET_CHUNK, H, a.shape[-1]).swapaxes(0, 1)

    def body(state, inp):
        qc, kc, vc = inp
        return retention_chunk(state, qc, kc, vc, log_g)

    s0 = jnp.zeros((B, H, A_DK, A_DV), jnp.float32)
    s_final, o = lax.scan(body, s0, (to_chunks(q), to_chunks(k), to_chunks(v)))
    return o.swapaxes(0, 1).reshape(B, S, H, A_DV), s_final


def retention_output(o, g, w_out):
    B, T = o.shape[:2]
    o = o * lax.rsqrt(jnp.mean(o * o, axis=-1, keepdims=True) + RMS_EPS)
    return (jax.nn.silu(g) * o.reshape(B, T, A_V).astype(g.dtype)) @ w_out


def b_project(h, w_in):
    B, T, _ = h.shape
    p = (h @ w_in).reshape(B, T, 3, N_GROUPS, B_HEADS_PER_GROUP, B_HEAD_DIM)
    return p[:, :, 0], p[:, :, 1], p[:, :, 2]


def dilated_band_prompt(q, k, v, dil, back):
    B, S, H, E = q.shape
    L = S // dil
    nb = -(-L // back)
    Lp = nb * back

    def residue_major(a):
        a = a.reshape(B, L, dil, H, E).transpose(0, 2, 1, 3, 4)
        a = jnp.pad(a, ((0, 0), (0, 0), (0, Lp - L), (0, 0), (0, 0)))
        return a.reshape(B, dil, nb, back, H, E)

    def with_prev(a):
        prev = jnp.pad(a, ((0, 0), (0, 0), (1, 0), (0, 0), (0, 0), (0, 0)))[:, :, :nb]
        return jnp.concatenate([prev, a], axis=3)

    qb = residue_major(q)
    kb = with_prev(residue_major(k))
    vb = with_prev(residue_major(v))
    s = jnp.einsum('brnqhe,brnkhe->brnhqk', qb, kb).astype(jnp.float32) * (E ** -0.5)
    qi = jnp.arange(back)[:, None]
    kj = jnp.arange(2 * back)[None, :]
    dist = back + qi - kj
    band = (dist >= 0) & (dist <= back)
    has_prev = (jnp.arange(nb) > 0)[:, None, None] | (kj >= back)[None]
    valid = band[None] & has_prev
    s = jnp.where(valid[None, None, :, None], s, NEG_INF)
    m = jnp.max(s, axis=-1, keepdims=True)
    p = jnp.exp(s - m)
    l = jnp.sum(p, axis=-1, keepdims=True)
    o = jnp.einsum('brnhqk,brnkhe->brnqhe', (p / l).astype(v.dtype), vb)
    lse = (m + jnp.log(l))[..., 0]
    o = o.reshape(B, dil, Lp, H, E)[:, :, :L].transpose(0, 2, 1, 3, 4).reshape(B, S, H, E)
    lse = lse.transpose(0, 1, 2, 4, 3).reshape(B, dil, Lp, H)[:, :, :L].transpose(0, 2, 1, 3).reshape(B, S, H)
    return o, lse


def dilated_band_sample(q, k_all, v_all, n_past, dil, back):
    T, E = q.shape[1], q.shape[-1]
    idx = n_past + jnp.arange(T)[:, None] - dil * jnp.arange(back + 1)[None, :]
    valid = idx >= 0
    idx = jnp.maximum(idx, 0)
    kg = k_all[:, idx]
    vg = v_all[:, idx]
    s = jnp.einsum('bthe,btmhe->bhtm', q, kg).astype(jnp.float32) * (E ** -0.5)
    s = jnp.where(valid[None, None], s, NEG_INF)
    m = jnp.max(s, axis=-1, keepdims=True)
    p = jnp.exp(s - m)
    l = jnp.sum(p, axis=-1, keepdims=True)
    o = jnp.einsum('bhtm,btmhe->bthe', (p / l).astype(vg.dtype), vg)
    lse = (m + jnp.log(l))[..., 0].transpose(0, 2, 1)
    return o, lse


def b_merge(o, lse, w_out):
    B, T = o.shape[:2]
    alpha = jax.nn.softmax(lse, axis=2)
    return (o * alpha[..., None].astype(o.dtype)).reshape(B, T, B_WIDTH) @ w_out


def dilated_mixer_prompt(h, w_in, w_out):
    q, k, v = b_project(h, w_in)
    T = h.shape[1]
    outs, lses, kv_rows = [], [], []
    for g, (win, dil) in enumerate(B_GROUPS):
        o, lse = dilated_band_prompt(q[:, :, g], k[:, :, g], v[:, :, g], dil, win // dil)
        outs.append(o)
        lses.append(lse)
        keep = min(win, T)
        kv_rows.append(jnp.stack([k[:, T - keep:, g], v[:, T - keep:, g]], axis=2))
    return b_merge(jnp.stack(outs, axis=2), jnp.stack(lses, axis=2), w_out), kv_rows


def dilated_mixer_sample(h, buffers, w_in, w_out):
    q, k, v = b_project(h, w_in)
    outs, lses, kv_rows = [], [], []
    for g, (win, dil) in enumerate(B_GROUPS):
        buf = buffers[g]
        k_all = jnp.concatenate([buf[:, :, 0], k[:, :, g]], axis=1)
        v_all = jnp.concatenate([buf[:, :, 1], v[:, :, g]], axis=1)
        o, lse = dilated_band_sample(q[:, :, g], k_all, v_all, buf.shape[1], dil, win // dil)
        outs.append(o)
        lses.append(lse)
        kv_rows.append(jnp.stack([k[:, :, g], v[:, :, g]], axis=2))
    return b_merge(jnp.stack(outs, axis=2), jnp.stack(lses, axis=2), w_out), kv_rows


def rwkv7_scan(state, r, dec, k, v, a_vec, b_vec):
    def step(S, inp):
        r_t, w_t, k_t, v_t, a_t, b_t = inp
        sa = jnp.einsum('bhvk,bhk->bhv', S, a_t)
        S = S * w_t[:, :, None, :] + sa[..., None] * b_t[:, :, None, :] + v_t[..., None] * k_t[:, :, None, :]
        return S, jnp.einsum('bhvk,bhk->bhv', S, r_t)

    xs = tuple(jnp.swapaxes(t, 0, 1) for t in (r, dec, k, v, a_vec, b_vec))
    S, y = lax.scan(step, state, xs)
    return S, jnp.swapaxes(y, 0, 1)


def rwkv7_mix(h, h_last, S0, mu, w_rkv, w0, w1, w2, a0, a1, a2, g1, g2, k_k, k_a, r_k, ln_w, ln_b, w_out):
    B, T, D = h.shape
    prev = jnp.concatenate([h_last[:, None].astype(h.dtype), h[:, :-1]], axis=1)
    xm = h[:, :, None, :] + (prev - h)[:, :, None, :] * mu
    rkv = jnp.einsum('btgd,gde->btge', xm[:, :, :3], w_rkv)
    r, k, v = rkv[:, :, 0], rkv[:, :, 1], rkv[:, :, 2]
    w = -jax.nn.softplus(-(w0 + jnp.tanh(xm[:, :, 3] @ w1) @ w2)) - 0.5
    a = jax.nn.sigmoid(a0 + (xm[:, :, 4] @ a1) @ a2)
    g = jax.nn.sigmoid(xm[:, :, 5] @ g1) @ g2

    def heads(t):
        return t.astype(jnp.float32).reshape(B, T, C_HEADS, C_HEAD)

    kk = heads(k * k_k)
    kk = kk / jnp.maximum(jnp.sqrt(jnp.sum(kk * kk, axis=-1, keepdims=True)), 1e-12)
    a_h = heads(a)
    k_h = heads(k * (1 + (a - 1) * k_a))
    r_h, v_h = heads(r), heads(v)
    dec = jnp.exp(-jnp.exp(heads(w)))
    S, y = rwkv7_scan(S0.astype(jnp.float32), r_h, dec, k_h, v_h, -kk, kk * a_h)
    mean = jnp.mean(y, axis=-1, keepdims=True)
    var = jnp.mean(jnp.square(y - mean), axis=-1, keepdims=True)
    y = ((y - mean) * lax.rsqrt(var + GN_EPS)).reshape(B, T, D) * ln_w + ln_b
    bonus = jnp.sum(r_h * k_h * r_k, axis=-1, keepdims=True) * v_h
    o = (y + bonus.reshape(B, T, D)).astype(h.dtype) * g
    return o @ w_out, S, h[:, -1]


def setup_inputs(seed: int = 0) -> dict:
    key = jax.random.key(seed)
    ks = iter(jax.random.split(key, 48))
    D = D_MODEL

    def nrm(shape, scale):
        return jax.random.normal(next(ks), shape, jnp.float32) * scale

    def gain(shape):
        return 1.0 + nrm(shape, 0.02)

    inp = {}
    inp['x_prompt'] = nrm((BATCH, SEQ, D), 1.0)
    inp['x_sample'] = nrm((DEC_BATCH, DEC_SEQ, D), 1.0)
    inp['state_a'] = nrm((N_LAYERS_A, DEC_BATCH, A_HEADS, A_DK, A_DV), 0.5)
    inp['cache_b0'] = nrm((N_LAYERS_B, DEC_BATCH, min(B_GROUPS[0][0], PAST_LEN), 2, B_HEADS_PER_GROUP, B_HEAD_DIM), 1.0)
    inp['cache_b1'] = nrm((N_LAYERS_B, DEC_BATCH, min(B_GROUPS[1][0], PAST_LEN), 2, B_HEADS_PER_GROUP, B_HEAD_DIM), 1.0)
    inp['cache_b2'] = nrm((N_LAYERS_B, DEC_BATCH, min(B_GROUPS[2][0], PAST_LEN), 2, B_HEADS_PER_GROUP, B_HEAD_DIM), 1.0)
    inp['state_c_wkv'] = nrm((N_LAYERS_C, DEC_BATCH, C_HEADS, C_HEAD, C_HEAD), 0.5)
    inp['state_c_shift'] = nrm((N_LAYERS_C, DEC_BATCH, D), 1.0)
    inp['norm_ffn1'] = gain((DEPTH, D))
    inp['ffn1_in'] = nrm((DEPTH, D, 2 * D_FF), D ** -0.5)
    inp['ffn1_out'] = nrm((DEPTH, D_FF, D), D_FF ** -0.5)
    inp['norm_mix'] = gain((DEPTH, D))
    inp['norm_ffn2'] = gain((DEPTH, D))
    inp['ffn2_in'] = nrm((DEPTH, D, 2 * D_FF), D ** -0.5)
    inp['ffn2_out'] = nrm((DEPTH, D_FF, D), D_FF ** -0.5)
    inp['norm_final'] = gain((D,))
    inp['a_in'] = nrm((N_LAYERS_A, D, 2 * A_QK + 2 * A_V), D ** -0.5)
    inp['a_out'] = nrm((N_LAYERS_A, A_V, D), A_V ** -0.5)
    inp['b_in'] = nrm((N_LAYERS_B, D, 3 * B_WIDTH), D ** -0.5)
    inp['b_out'] = nrm((N_LAYERS_B, B_WIDTH, D), B_WIDTH ** -0.5)
    inp['c_mu'] = jax.random.uniform(next(ks), (N_LAYERS_C, 6, D), jnp.float32)
    inp['c_rkv'] = nrm((N_LAYERS_C, 3, D, D), D ** -0.5)
    inp['c_w0'] = jax.random.uniform(next(ks), (N_LAYERS_C, D), jnp.float32, -3.0, 0.5)
    inp['c_w1'] = nrm((N_LAYERS_C, D, C_DECAY_LORA), D ** -0.5)
    inp['c_w2'] = nrm((N_LAYERS_C, C_DECAY_LORA, D), 0.1 * C_DECAY_LORA ** -0.5)
    inp['c_a0'] = nrm((N_LAYERS_C, D), 0.5)
    inp['c_a1'] = nrm((N_LAYERS_C, D, C_AAA_LORA), D ** -0.5)
    inp['c_a2'] = nrm((N_LAYERS_C, C_AAA_LORA, D), 0.5 * C_AAA_LORA ** -0.5)
    inp['c_g1'] = nrm((N_LAYERS_C, D, C_GATE_LORA), D ** -0.5)
    inp['c_g2'] = nrm((N_LAYERS_C, C_GATE_LORA, D), C_GATE_LORA ** -0.5)
    inp['c_kk'] = 0.85 + nrm((N_LAYERS_C, D), 0.02)
    inp['c_ka'] = 1.0 + nrm((N_LAYERS_C, D), 0.02)
    inp['c_rk'] = nrm((N_LAYERS_C, C_HEADS, C_HEAD), 0.1)
    inp['c_ln_w'] = gain((N_LAYERS_C, D))
    inp['c_ln_b'] = nrm((N_LAYERS_C, D), 0.02)
    inp['c_out'] = nrm((N_LAYERS_C, D, D), D ** -0.5)
    return inp


def reference(x_prompt, x_sample, state_a, cache_b0, cache_b1, cache_b2, state_c_wkv, state_c_shift,
              norm_ffn1, ffn1_in, ffn1_out, norm_mix, norm_ffn2, ffn2_in, ffn2_out, norm_final,
              a_in, a_out, b_in, b_out,
              c_mu, c_rkv, c_w0, c_w1, c_w2, c_a0, c_a1, c_a2, c_g1, c_g2, c_kk, c_ka, c_rk,
              c_ln_w, c_ln_b, c_out):
    cache_b = (cache_b0, cache_b1, cache_b2)
    xp, xd = x_prompt, x_sample
    pos_p = jnp.arange(xp.shape[1])
    pos_d = PAST_LEN + jnp.arange(xd.shape[1])
    log_g = retention_log_decay()
    a_p, a_d = [], []
    b_p = [[] for _ in B_GROUPS]
    b_d = [[] for _ in B_GROUPS]
    c_wkv_p, c_sh_p, c_wkv_d, c_sh_d = [], [], [], []
    for i in range(DEPTH):
        kind, j = i % N_MIXERS, i // N_MIXERS
        xp = xp + 0.5 * swiglu(rmsnorm(xp, norm_ffn1[i]), ffn1_in[i], ffn1_out[i])
        xd = xd + 0.5 * swiglu(rmsnorm(xd, norm_ffn1[i]), ffn1_in[i], ffn1_out[i])
        hp, hd = rmsnorm(xp, norm_mix[i]), rmsnorm(xd, norm_mix[i])
        if kind == 0:
            q, k, v, g = retention_project(hp, a_in[j], pos_p)
            o, s_new = retention_prompt(q, k, v, log_g)
            mp = retention_output(o, g, a_out[j])
            a_p.append(s_new.astype(xp.dtype))
            q, k, v, g = retention_project(hd, a_in[j], pos_d)
            s_new, o = retention_chunk(state_a[j].astype(jnp.float32), q, k, v, log_g)
            md = retention_output(o, g, a_out[j])
            a_d.append(s_new.astype(xd.dtype))
        elif kind == 1:
            mp, rows_p = dilated_mixer_prompt(hp, b_in[j], b_out[j])
            md, rows_d = dilated_mixer_sample(hd, [c[j] for c in cache_b], b_in[j], b_out[j])
            for gi in range(N_GROUPS):
                b_p[gi].append(rows_p[gi])
                b_d[gi].append(rows_d[gi])
        else:
            c_par = (c_mu[j], c_rkv[j], c_w0[j], c_w1[j], c_w2[j], c_a0[j], c_a1[j], c_a2[j], c_g1[j], c_g2[j],
                     c_kk[j], c_ka[j], c_rk[j], c_ln_w[j], c_ln_b[j], c_out[j])
            zero_shift = jnp.zeros((hp.shape[0], D_MODEL), hp.dtype)
            zero_state = jnp.zeros((hp.shape[0], C_HEADS, C_HEAD, C_HEAD), jnp.float32)
            mp, s_p, sh_p = rwkv7_mix(hp, zero_shift, zero_state, *c_par)
            md, s_d, sh_d = rwkv7_mix(hd, state_c_shift[j], state_c_wkv[j], *c_par)
            c_wkv_p.append(s_p.astype(xp.dtype))
            c_sh_p.append(sh_p)
            c_wkv_d.append(s_d.astype(xd.dtype))
            c_sh_d.append(sh_d)
        xp = xp + mp
        xd = xd + md
        xp = xp + 0.5 * swiglu(rmsnorm(xp, norm_ffn2[i]), ffn2_in[i], ffn2_out[i])
        xd = xd + 0.5 * swiglu(rmsnorm(xd, norm_ffn2[i]), ffn2_in[i], ffn2_out[i])
    y_prompt = rmsnorm(xp, norm_final)
    y_sample = rmsnorm(xd, norm_final)
    return (y_prompt, y_sample,
            jnp.stack(a_p), jnp.stack(a_d),
            jnp.stack(b_p[0]), jnp.stack(b_p[1]), jnp.stack(b_p[2]),
            jnp.stack(b_d[0]), jnp.stack(b_d[1]), jnp.stack(b_d[2]),
            jnp.stack(c_wkv_p), jnp.stack(c_sh_p), jnp.stack(c_wkv_d), jnp.stack(c_sh_d))
```

```python
import functools

import jax
import jax.numpy as jnp
from jax import lax
from jax.experimental import pallas as pl
from jax.experimental.pallas import tpu as pltpu

F32 = jnp.float32
BF16 = jnp.bfloat16

D_MODEL = 2048
DEPTH = 4
PAST_LEN = 16384
N_MIXERS = 3
RMS_EPS = 1e-6
A_DK = 256
A_HEADS = D_MODEL // A_DK
A_DV = 2 * A_DK
A_QK = A_HEADS * A_DK
A_V = A_HEADS * A_DV
RET_CHUNK = 128
ROPE_BASE = 10000.0
B_GROUPS = ((128, 1), (512, 4), (2048, 16))
N_GROUPS = len(B_GROUPS)
B_HEAD_DIM = 128
B_HEADS_PER_GROUP = D_MODEL // 512
B_GROUP_WIDTH = B_HEADS_PER_GROUP * B_HEAD_DIM
B_WIDTH = N_GROUPS * B_GROUP_WIDTH
B_BACK = 128
NEG_INF = -1e30
C_HEAD = 64
C_HEADS = D_MODEL // C_HEAD
GN_EPS = 64e-5

LANES = 128
SUBLANES = 8
VMEM_LIMIT_BYTES = 56 * 1024 * 1024
SAMPLE_ROWS = 16
SCAN_BLOCK = 64
C_PAIRS = C_HEADS // 2


def _params(sem):
    return pltpu.CompilerParams(dimension_semantics=sem, vmem_limit_bytes=VMEM_LIMIT_BYTES)


def _rms(x, g):
    r = lax.rsqrt(jnp.mean(x * x, axis=-1, keepdims=True) + RMS_EPS)
    return (x * r) * g


def _dot(a, b):
    return jnp.dot(a, b, preferred_element_type=F32)


def _dot_nt(a, b):
    return lax.dot_general(a, b, (((1,), (1,)), ((), ())), preferred_element_type=F32)


def _dot_tn(a, b):
    return lax.dot_general(a, b, (((0,), (0,)), ((), ())), preferred_element_type=F32)


def _ffn_body(x_ref, g_ref, wg_ref, wu_ref, wo_ref, *rest, n_chunks, final):
    if final:
        gf_ref, o_ref, y_ref, xn_ref, acc_ref = rest
    else:
        o_ref, xn_ref, acc_ref = rest
    j = pl.program_id(1)

    @pl.when(j == 0)
    def _():
        xn_ref[...] = _rms(x_ref[...], g_ref[...]).astype(BF16)
        acc_ref[...] = jnp.zeros_like(acc_ref)

    xn = xn_ref[...]
    gate = _dot(xn, wg_ref[...])
    up = _dot(xn, wu_ref[...])
    h = (jax.nn.silu(gate) * up).astype(BF16)
    acc_ref[...] += _dot(h, wo_ref[...])

    @pl.when(j == n_chunks - 1)
    def _():
        xo = x_ref[...] + 0.5 * acc_ref[...]
        o_ref[...] = xo
        if final:
            y_ref[...] = _rms(xo, gf_ref[...])


def _ffn(x, gain, w_in, w_out, *, tm, tf=512, final_gain=None):
    m, d = x.shape
    f = w_out.shape[0]
    nf = f // tf
    final = final_gain is not None
    in_specs = [
        pl.BlockSpec((tm, d), lambda i, j: (i, 0)),
        pl.BlockSpec((1, d), lambda i, j: (0, 0)),
        pl.BlockSpec((d, tf), lambda i, j: (0, j)),
        pl.BlockSpec((d, tf), lambda i, j: (0, j + nf)),
        pl.BlockSpec((tf, d), lambda i, j: (j, 0)),
    ]
    args = [x, gain.reshape(1, d), w_in, w_in, w_out]
    out_shape = [jax.ShapeDtypeStruct((m, d), F32)]
    out_specs = [pl.BlockSpec((tm, d), lambda i, j: (i, 0))]
    if final:
        in_specs.append(pl.BlockSpec((1, d), lambda i, j: (0, 0)))
        args.append(final_gain.reshape(1, d))
        out_shape.append(jax.ShapeDtypeStruct((m, d), F32))
        out_specs.append(pl.BlockSpec((tm, d), lambda i, j: (i, 0)))
    res = pl.pallas_call(
        functools.partial(_ffn_body, n_chunks=nf, final=final),
        grid=(m // tm, nf),
        in_specs=in_specs,
        out_specs=out_specs,
        out_shape=out_shape,
        scratch_shapes=[pltpu.VMEM((tm, d), BF16), pltpu.VMEM((tm, d), F32)],
        compiler_params=_params(("parallel", "arbitrary")),
        name="ffn",
    )(*args)
    return res if final else res[0]


def _norm_mm_body(x_ref, g_ref, w_ref, o_ref, xn_ref):
    @pl.when(pl.program_id(1) == 0)
    def _():
        xn_ref[...] = _rms(x_ref[...], g_ref[...]).astype(BF16)

    o_ref[...] = _dot(xn_ref[...], w_ref[...])


def _norm_mm(x, gain, w, *, tm, tn):
    m, d = x.shape
    n = w.shape[1]
    return pl.pallas_call(
        _norm_mm_body,
        grid=(m // tm, n // tn),
        in_specs=[
            pl.BlockSpec((tm, d), lambda i, j: (i, 0)),
            pl.BlockSpec((1, d), lambda i, j: (0, 0)),
            pl.BlockSpec((d, tn), lambda i, j: (0, j)),
        ],
        out_specs=pl.BlockSpec((tm, tn), lambda i, j: (i, j)),
        out_shape=jax.ShapeDtypeStruct((m, n), F32),
        scratch_shapes=[pltpu.VMEM((tm, d), BF16)],
        compiler_params=_params(("parallel", "arbitrary")),
        name="norm_mm",
    )(x, gain.reshape(1, d), w)


def _mm_res_body(*refs, n_in, prologue):
    ins = refs[:n_in]
    w_ref, x_ref, o_ref, a_ref = refs[n_in:]

    @pl.when(pl.program_id(1) == 0)
    def _():
        a_ref[...] = prologue(*[r[...] for r in ins]).astype(BF16)

    o_ref[...] = x_ref[...] + _dot(a_ref[...], w_ref[...])


def _mm_res(ins, in_row_tiled, prologue, w, x, *, tm, tn, name):
    m, n = x.shape
    k = w.shape[0]
    in_specs = []
    for a, tiled in zip(ins, in_row_tiled):
        if tiled:
            in_specs.append(pl.BlockSpec((tm, a.shape[1]), lambda i, j: (i, 0)))
        else:
            in_specs.append(pl.BlockSpec((1, a.shape[1]), lambda i, j: (0, 0)))
    in_specs += [
        pl.BlockSpec((k, tn), lambda i, j: (0, j)),
        pl.BlockSpec((tm, tn), lambda i, j: (i, j)),
    ]
    return pl.pallas_call(
        functools.partial(_mm_res_body, n_in=len(ins), prologue=prologue),
        grid=(m // tm, n // tn),
        in_specs=in_specs,
        out_specs=pl.BlockSpec((tm, tn), lambda i, j: (i, j)),
        out_shape=jax.ShapeDtypeStruct((m, n), F32),
        scratch_shapes=[pltpu.VMEM((tm, k), BF16)],
        compiler_params=_params(("parallel", "arbitrary")),
        name=name,
    )(*ins, w, x)


def _pad_rows(x, rows):
    if x.shape[0] == rows:
        return x
    return jnp.concatenate([x, jnp.zeros((rows - x.shape[0], x.shape[1]), x.dtype)], axis=0)


def _ret_body(lg_ref, q_ref, k_ref, v_ref, g_ref, cos_ref, sin_ref, s0_ref, og_ref, s_ref, *, c_true):
    h = pl.program_id(1)
    c = pl.program_id(2)

    @pl.when(c == 0)
    def _():
        s_ref[...] = s0_ref[...]

    lg = lg_ref[h]
    rows = q_ref.shape[0]
    cp = RET_CHUNK
    half = A_DK // 2
    cos = _pad_rows(cos_ref[...], cp)
    sin = _pad_rows(sin_ref[...], cp)

    def rot(x):
        x1, x2 = x[:, :half], x[:, half:]
        return jnp.concatenate([x1 * cos - x2 * sin, x1 * sin + x2 * cos], axis=-1)

    q = rot(_pad_rows(q_ref[...], cp))
    k = rot(_pad_rows(k_ref[...], cp)) * (A_DK ** -0.5)
    vb = _pad_rows(v_ref[...], cp).astype(BF16)
    row = lax.broadcasted_iota(jnp.int32, (cp, 1), 0).astype(F32)
    if c_true < cp:
        k = jnp.where(row < c_true, k, 0.0)
    qi = lax.broadcasted_iota(jnp.int32, (cp, cp), 0)
    kj = lax.broadcasted_iota(jnp.int32, (cp, cp), 1)
    dist = (qi - kj).astype(F32)
    intra = jnp.where(dist >= 0, jnp.exp(jnp.maximum(dist, 0.0) * lg), 0.0)
    s = _dot_nt(q.astype(BF16), k.astype(BF16)) * intra
    state = s_ref[...]
    q_dec = jnp.exp((row + 1.0) * lg)
    o = _dot(s.astype(BF16), vb) + _dot((q * q_dec).astype(BF16), state.astype(BF16))
    k_dec = jnp.exp((c_true - 1.0 - row) * lg)
    chunk_dec = jnp.exp(jnp.full((1, 1), float(c_true), F32) * lg)
    s_ref[...] = chunk_dec * state + _dot_tn((k * k_dec).astype(BF16), vb)
    o = o[:rows]
    on = o * lax.rsqrt(jnp.mean(o * o, axis=-1, keepdims=True) + RMS_EPS)
    og_ref[...] = (jax.nn.silu(g_ref[...]) * on).astype(BF16)


def _retention(p, s0, pos0, *, n_seq, n_chunks, rows, c_true):
    log_g = jnp.log1p(-jnp.exp2(-5.0 - jnp.arange(A_HEADS, dtype=F32)))
    half = A_DK // 2
    inv = ROPE_BASE ** (-jnp.arange(half, dtype=F32) / half)
    pos = (pos0 + jnp.arange(n_chunks * rows)).astype(F32)
    ang = pos[:, None] * inv[None, :]
    cos = jnp.cos(ang).reshape(n_chunks, rows, half)
    sin = jnp.sin(ang).reshape(n_chunks, rows, half)
    nqk = A_QK // A_DK
    nv = 2 * A_QK // A_DV
    grid = (n_seq, A_HEADS, n_chunks)
    og, s_new = pl.pallas_call(
        functools.partial(_ret_body, c_true=c_true),
        grid=grid,
        in_specs=[
            pl.BlockSpec(memory_space=pltpu.SMEM),
            pl.BlockSpec((None, rows, A_DK), lambda b, h, c: (b * n_chunks + c, 0, h)),
            pl.BlockSpec((None, rows, A_DK), lambda b, h, c: (b * n_chunks + c, 0, nqk + h)),
            pl.BlockSpec((None, rows, A_DV), lambda b, h, c: (b * n_chunks + c, 0, nv + h)),
            pl.BlockSpec((None, rows, A_DV), lambda b, h, c: (b * n_chunks + c, 0, nv + A_HEADS + h)),
            pl.BlockSpec((None, rows, half), lambda b, h, c: (c, 0, 0)),
            pl.BlockSpec((None, rows, half), lambda b, h, c: (c, 0, 0)),
            pl.BlockSpec((None, None, A_DK, A_DV), lambda b, h, c: (b, h, 0, 0)),
        ],
        out_specs=[
            pl.BlockSpec((None, rows, A_DV), lambda b, h, c: (b * n_chunks + c, 0, h)),
            pl.BlockSpec((None, None, A_DK, A_DV), lambda b, h, c: (b, h, 0, 0)),
        ],
        out_shape=[
            jax.ShapeDtypeStruct((n_seq * n_chunks, rows, A_V), BF16),
            jax.ShapeDtypeStruct((n_seq, A_HEADS, A_DK, A_DV), F32),
        ],
        compiler_params=_params(("parallel", "parallel", "arbitrary")),
        name="retention",
    )(log_g, p, p, p, p, cos, sin, s0)
    return og, s_new


def _dil_body(q_ref, kp_ref, ko_ref, vp_ref, vo_ref, o_ref, lse_ref):
    n = pl.program_id(2)
    scale = B_HEAD_DIM ** -0.5
    q = q_ref[...].astype(BF16)
    sp = _dot_nt(q, kp_ref[...].astype(BF16)) * scale
    so = _dot_nt(q, ko_ref[...].astype(BF16)) * scale
    qi = lax.broadcasted_iota(jnp.int32, (B_BACK, B_BACK), 0)
    kj = lax.broadcasted_iota(jnp.int32, (B_BACK, B_BACK), 1)
    sp = jnp.where(kj >= qi + jnp.where(n > 0, 0, B_BACK), sp, NEG_INF)
    so = jnp.where(kj <= qi, so, NEG_INF)
    m = jnp.maximum(jnp.max(sp, axis=-1, keepdims=True), jnp.max(so, axis=-1, keepdims=True))
    pp = jnp.exp(sp - m)
    po = jnp.exp(so - m)
    l = jnp.sum(pp, axis=-1, keepdims=True) + jnp.sum(po, axis=-1, keepdims=True)
    o = _dot(pp.astype(BF16), vp_ref[...].astype(BF16)) + _dot(po.astype(BF16), vo_ref[...].astype(BF16))
    o_ref[...] = o / l
    lse_ref[...] = jnp.broadcast_to(m + jnp.log(l), lse_ref.shape)


def _dilated_prompt(p, g, dil, *, n_seq, seq):
    hg, e = B_HEADS_PER_GROUP, B_HEAD_DIM
    sub = seq // dil
    nb = sub // B_BACK
    pcols = 3 * B_WIDTH // e
    pv = p.reshape(n_seq * sub, dil * 3 * B_WIDTH)
    kcol = B_WIDTH // e
    q_map = lambda b, r, n, h: (b * nb + n, r * pcols + g * hg + h)
    kp_map = lambda b, r, n, h: (jnp.maximum(b * nb + n - 1, 0), r * pcols + kcol + g * hg + h)
    ko_map = lambda b, r, n, h: (b * nb + n, r * pcols + kcol + g * hg + h)
    vp_map = lambda b, r, n, h: (jnp.maximum(b * nb + n - 1, 0), r * pcols + 2 * kcol + g * hg + h)
    vo_map = lambda b, r, n, h: (b * nb + n, r * pcols + 2 * kcol + g * hg + h)
    o_map = lambda b, r, n, h: (b * nb + n, r * hg + h)
    blk = (B_BACK, e)
    o, lse = pl.pallas_call(
        _dil_body,
        grid=(n_seq, dil, nb, hg),
        in_specs=[pl.BlockSpec(blk, q_map), pl.BlockSpec(blk, kp_map), pl.BlockSpec(blk, ko_map),
                  pl.BlockSpec(blk, vp_map), pl.BlockSpec(blk, vo_map)],
        out_specs=[pl.BlockSpec(blk, o_map), pl.BlockSpec(blk, o_map)],
        out_shape=[jax.ShapeDtypeStruct((n_seq * sub, dil * hg * e), F32)] * 2,
        compiler_params=_params(("parallel", "parallel", "arbitrary", "arbitrary")),
        name="dilated_prompt",
    )(pv, pv, pv, pv, pv)
    return o.reshape(n_seq * seq, hg * e), lse.reshape(n_seq * seq, hg * e)


def _dil_sample_body(p_ref, c_ref, o_ref, lse_ref, *, g, dil, n_new):
    rows = p_ref.shape[0]
    n_past = c_ref.shape[0]
    hg, e = B_HEADS_PER_GROUP, B_HEAD_DIM
    scale = e ** -0.5
    t = lax.broadcasted_iota(jnp.int32, (rows, n_past), 0)
    idx = lax.broadcasted_iota(jnp.int32, (rows, n_past), 1)
    valid = jnp.logical_and(idx >= t, ((idx - t) & (dil - 1)) == 0)
    tn = lax.broadcasted_iota(jnp.int32, (rows, 1), 0)
    for h in range(hg):
        col = (g * hg + h) * e
        q = p_ref[:, col:col + e]
        kn = p_ref[:, B_WIDTH + col:B_WIDTH + col + e]
        vn = p_ref[:, 2 * B_WIDTH + col:2 * B_WIDTH + col + e]
        kc = c_ref[:, h * e:(h + 1) * e]
        vc = c_ref[:, (hg + h) * e:(hg + h + 1) * e]
        s = jnp.where(valid, _dot_nt(q.astype(BF16), kc.astype(BF16)) * scale, NEG_INF)
        m = jnp.max(s, axis=-1, keepdims=True)
        s_new = []
        for j in range(n_new):
            sj = jnp.sum(q * kn[j:j + 1, :], axis=-1, keepdims=True) * scale
            vj = jnp.logical_and(tn >= j, ((tn - j) & (dil - 1)) == 0)
            sj = jnp.where(vj, sj, NEG_INF)
            s_new.append(sj)
            m = jnp.maximum(m, sj)
        pr = jnp.exp(s - m)
        l = jnp.sum(pr, axis=-1, keepdims=True)
        o = _dot(pr.astype(BF16), vc.astype(BF16))
        for j in range(n_new):
            pj = jnp.exp(s_new[j] - m)
            l = l + pj
            o = o + pj * vn[j:j + 1, :]
        o_ref[:, h * e:(h + 1) * e] = o / l
        lse_ref[:, h * e:(h + 1) * e] = jnp.broadcast_to(m + jnp.log(l), (rows, e))


def _dilated_sample(p, cache, g, dil, *, n_new):
    n_seq, rows, width = p.shape
    n_past = cache.shape[1]
    gw = B_GROUP_WIDTH
    return pl.pallas_call(
        functools.partial(_dil_sample_body, g=g, dil=dil, n_new=n_new),
        grid=(n_seq,),
        in_specs=[pl.BlockSpec((None, rows, width), lambda b: (b, 0, 0)),
                  pl.BlockSpec((None, n_past, 2 * gw), lambda b: (b, 0, 0))],
        out_specs=[pl.BlockSpec((None, rows, gw), lambda b: (b, 0, 0))] * 2,
        out_shape=[jax.ShapeDtypeStruct((n_seq, rows, gw), F32)] * 2,
        compiler_params=_params(("parallel",)),
        name="dilated_sample",
    )(p, cache)


def _merge_prologue(o0, o1, o2, l0, l1, l2):
    m = jnp.maximum(jnp.maximum(l0, l1), l2)
    e0, e1, e2 = jnp.exp(l0 - m), jnp.exp(l1 - m), jnp.exp(l2 - m)
    tot = e0 + e1 + e2
    return jnp.concatenate([o0 * (e0 / tot), o1 * (e1 / tot), o2 * (e2 / tot)], axis=-1)


def _norm_body(x_ref, g_ref, o_ref):
    o_ref[...] = _rms(x_ref[...], g_ref[...])


def _norm(x, gain, *, tm):
    m, d = x.shape
    return pl.pallas_call(
        _norm_body,
        grid=(m // tm,),
        in_specs=[pl.BlockSpec((tm, d), lambda i: (i, 0)), pl.BlockSpec((1, d), lambda i: (0, 0))],
        out_specs=pl.BlockSpec((tm, d), lambda i: (i, 0)),
        out_shape=jax.ShapeDtypeStruct((m, d), F32),
        compiler_params=_params(("parallel",)),
        name="norm",
    )(x, gain.reshape(1, d))


def _rkv_body(h_ref, p_ref, mu_ref, w_ref, o_ref, xm_ref):
    @pl.when(pl.program_id(2) == 0)
    def _():
        h = h_ref[...]
        xm_ref[...] = (h + (p_ref[...] - h) * mu_ref[...]).astype(BF16)

    o_ref[...] = _dot(xm_ref[...], w_ref[...])


def _rkv(h, prev, mu, w, *, tm, tn):
    m, d = h.shape
    n = w.shape[2]
    return pl.pallas_call(
        _rkv_body,
        grid=(m // tm, 3, n // tn),
        in_specs=[
            pl.BlockSpec((tm, d), lambda i, g, j: (i, 0)),
            pl.BlockSpec((tm, d), lambda i, g, j: (i, 0)),
            pl.BlockSpec((None, 1, d), lambda i, g, j: (g, 0, 0)),
            pl.BlockSpec((None, d, tn), lambda i, g, j: (g, 0, j)),
        ],
        out_specs=pl.BlockSpec((None, tm, tn), lambda i, g, j: (g, i, j)),
        out_shape=jax.ShapeDtypeStruct((3, m, n), F32),
        scratch_shapes=[pltpu.VMEM((tm, d), BF16)],
        compiler_params=_params(("parallel", "arbitrary", "arbitrary")),
        name="rwkv_rkv",
    )(h, prev, mu, w)


def _head_ones():
    ri = lax.broadcasted_iota(jnp.int32, (LANES, LANES), 0)
    ci = lax.broadcasted_iota(jnp.int32, (LANES, LANES), 1)
    same_head = (ri & C_HEAD) == (ci & C_HEAD)
    return jnp.where(same_head, 1.0, 0.0).astype(BF16)


def _head_sum(x, ones_bd):
    outs = []
    for t in range(x.shape[1] // LANES):
        xt = x[:, t * LANES:(t + 1) * LANES]
        hi = xt.astype(BF16)
        lo = (xt - hi.astype(F32)).astype(BF16)
        outs.append(_dot(hi, ones_bd) + _dot(lo, ones_bd))
    return jnp.concatenate(outs, axis=-1)


def _softplus(x):
    return jnp.maximum(x, 0.0) + jnp.log1p(jnp.exp(-jnp.abs(x)))


def _prep_body(h_ref, p_ref, rkv_ref, mu_ref, w0_ref, a0_ref, kk_ref, ka_ref, rk_ref,
               w1_ref, w2_ref, a1_ref, a2_ref, g1_ref, g2_ref,
               dec_ref, kh_ref, av_ref, bv_ref, gate_ref, bonus_ref):
    h = h_ref[...]
    d = p_ref[...] - h
    xw = (h + d * mu_ref[0]).astype(BF16)
    xa = (h + d * mu_ref[1]).astype(BF16)
    xg = (h + d * mu_ref[2]).astype(BF16)
    w = w0_ref[...] + _dot(jnp.tanh(_dot(xw, w1_ref[...])).astype(BF16), w2_ref[...])
    w = -_softplus(-w) - 0.5
    dec_ref[...] = jnp.exp(-jnp.exp(w))
    a = jax.nn.sigmoid(a0_ref[...] + _dot(_dot(xa, a1_ref[...]).astype(BF16), a2_ref[...]))
    gate_ref[...] = _dot(jax.nn.sigmoid(_dot(xg, g1_ref[...])).astype(BF16), g2_ref[...])
    r, k, v = rkv_ref[0], rkv_ref[1], rkv_ref[2]
    ones_bd = _head_ones()
    kk = k * kk_ref[...]
    kk = kk / jnp.maximum(jnp.sqrt(_head_sum(kk * kk, ones_bd)), 1e-12)
    kh = k * (1.0 + (a - 1.0) * ka_ref[...])
    kh_ref[...] = kh
    bonus_ref[...] = _head_sum(r * kh * rk_ref[...], ones_bd) * v
    av_ref[...] = -kk
    bv_ref[...] = kk * a


def _prep(h, prev, rkv, mu, w0, a0, kk, ka, rk, w1, w2, a1, a2, g1, g2, *, tm):
    m, d = h.shape
    row = pl.BlockSpec((tm, d), lambda i: (i, 0))
    vec = pl.BlockSpec((1, d), lambda i: (0, 0))

    def full(a):
        return pl.BlockSpec(a.shape, lambda i: (0,) * a.ndim)

    return pl.pallas_call(
        _prep_body,
        grid=(m // tm,),
        in_specs=[row, row, pl.BlockSpec((3, tm, d), lambda i: (0, i, 0)), full(mu),
                  vec, vec, vec, vec, vec,
                  full(w1), full(w2), full(a1), full(a2), full(g1), full(g2)],
        out_specs=[row] * 6,
        out_shape=[jax.ShapeDtypeStruct((m, d), F32)] * 6,
        compiler_params=_params(("parallel",)),
        name="rwkv_prep",
    )(h, prev, rkv, mu, w0.reshape(1, d), a0.reshape(1, d), kk.reshape(1, d), ka.reshape(1, d),
      rk.reshape(1, d), w1, w2, a1, a2, g1, g2)


def _scan_body(r_ref, w_ref, k_ref, a_ref, b_ref, vt_ref, s0_ref, y_ref, s_ref, yacc_ref, *, n_steps):
    @pl.when(pl.program_id(1) == 0)
    def _():
        s_ref[...] = s0_ref[...]

    yacc_ref[...] = jnp.zeros_like(yacc_ref)
    ones_bd = _head_ones()
    lane_t = lax.broadcasted_iota(jnp.int32, (C_HEAD, LANES), 1) & (C_HEAD - 1)
    rows = r_ref.shape[0]

    def group(base, steps):
        for p in range(C_PAIRS):
            sl = slice(p * LANES, (p + 1) * LANES)
            rows8 = pl.ds(base, SUBLANES)
            a8, w8, b8, k8, r8 = (ref[rows8, sl] for ref in (a_ref, w_ref, b_ref, k_ref, r_ref))
            state = s_ref[p]
            vt = vt_ref[p]
            yacc = yacc_ref[p]
            for j in range(steps):
                tmask = lane_t == base + j
                sa = _dot((state * a8[j:j + 1]).astype(BF16), ones_bd)
                vcol = _dot(jnp.where(tmask, vt, 0.0).astype(BF16), ones_bd)
                state = state * w8[j:j + 1] + sa * b8[j:j + 1] + vcol * k8[j:j + 1]
                yb = _dot((state * r8[j:j + 1]).astype(BF16), ones_bd)
                yacc = jnp.where(tmask, yb, yacc)
            s_ref[p] = state
            yacc_ref[p] = yacc

    def full_group(tg, carry):
        group(pl.multiple_of(tg * SUBLANES, SUBLANES), SUBLANES)
        return carry

    n_full, rem = divmod(n_steps, SUBLANES)
    if n_full:
        lax.fori_loop(0, n_full, full_group, 0)
    if rem:
        group(n_full * SUBLANES, rem)

    for p in range(C_PAIRS):
        y = yacc_ref[p]
        mean = jnp.mean(y, axis=0, keepdims=True)
        yc = y - mean
        var = jnp.mean(yc * yc, axis=0, keepdims=True)
        yn = yc * lax.rsqrt(var + GN_EPS)
        yt = jnp.concatenate([yn, jnp.zeros_like(yn)], axis=0).T
        nat = jnp.concatenate([yt[:C_HEAD, :C_HEAD], yt[C_HEAD:, :C_HEAD]], axis=1)
        y_ref[:, p * LANES:(p + 1) * LANES] = nat[:rows]


def _scan(r, dec, kh, av, bv, vt, s0, *, n_seq, n_blocks, rows, n_steps):
    d = r.shape[-1]
    seq = pl.BlockSpec((None, rows, d), lambda b, n: (b * n_blocks + n, 0, 0))
    st = pl.BlockSpec((None, C_PAIRS, C_HEAD, LANES), lambda b, n: (b, 0, 0, 0))
    return pl.pallas_call(
        functools.partial(_scan_body, n_steps=n_steps),
        grid=(n_seq, n_blocks),
        in_specs=[seq] * 5 + [pl.BlockSpec((None, C_PAIRS, C_HEAD, LANES), lambda b, n: (b * n_blocks + n, 0, 0, 0)), st],
        out_specs=[seq, st],
        out_shape=[jax.ShapeDtypeStruct((n_seq * n_blocks, rows, d), F32),
                   jax.ShapeDtypeStruct((n_seq, C_PAIRS, C_HEAD, LANES), F32)],
        scratch_shapes=[pltpu.VMEM((C_PAIRS, C_HEAD, LANES), F32)],
        compiler_params=_params(("parallel", "arbitrary")),
        name="rwkv_scan",
    )(r, dec, kh, av, bv, vt, s0)


def _rwkv_out_prologue(yn, bonus, gate, ln_w, ln_b):
    return ((yn * ln_w + ln_b) + bonus) * gate


def _pair_state(s):
    n = s.shape[0]
    s = s.reshape(n, C_PAIRS, 2, C_HEAD, C_HEAD).transpose(0, 1, 3, 2, 4)
    return s.reshape(n, C_PAIRS, C_HEAD, LANES)


def _unpair_state(s):
    n = s.shape[0]
    s = s.reshape(n, C_PAIRS, C_HEAD, 2, C_HEAD).transpose(0, 1, 3, 2, 4)
    return s.reshape(n, C_HEADS, C_HEAD, C_HEAD)


def _rwkv_mixer(x, h_last, s0, par, *, n_seq, rows_per_seq, n_valid, tm):
    (g_mix, mu, w_rkv, w0, w1, w2, a0, a1, a2, g1, g2, k_k, k_a, r_k, ln_w, ln_b, w_out) = par
    m, d = x.shape
    h = _norm(x, g_mix, tm=tm)
    h3 = h.reshape(n_seq, rows_per_seq, d)
    prev = jnp.concatenate([h_last[:, None, :], h3[:, :-1]], axis=1).reshape(m, d)
    rkv = _rkv(h, prev, mu[:3].reshape(3, 1, d), w_rkv, tm=tm, tn=1024)
    dec, kh, av, bv, gate, bonus = _prep(
        h, prev, rkv, mu[3:].reshape(3, 1, d), w0, a0, k_k, k_a, r_k.reshape(d),
        w1, w2, a1, a2, g1, g2, tm=min(tm, 128))
    if rows_per_seq % SCAN_BLOCK == 0:
        rows, n_blocks, n_steps = SCAN_BLOCK, rows_per_seq // SCAN_BLOCK, SCAN_BLOCK
    else:
        rows, n_blocks, n_steps = rows_per_seq, 1, n_valid
    vt = rkv[2].reshape(n_seq * n_blocks, rows, C_PAIRS, 2, C_HEAD).transpose(0, 2, 4, 3, 1)
    if rows < SCAN_BLOCK:
        vt = jnp.pad(vt, ((0, 0),) * 4 + ((0, SCAN_BLOCK - rows),))
    vt = vt.reshape(n_seq * n_blocks, C_PAIRS, C_HEAD, LANES)

    def blocks(a):
        return a.reshape(n_seq * n_blocks, rows, d)

    yn, s_new = _scan(blocks(rkv[0]), blocks(dec), blocks(kh), blocks(av), blocks(bv), vt,
                      _pair_state(s0), n_seq=n_seq, n_blocks=n_blocks, rows=rows, n_steps=n_steps)
    x_new = _mm_res([yn.reshape(m, d), bonus, gate, ln_w.reshape(1, d), ln_b.reshape(1, d)],
                    [True, True, True, False, False], _rwkv_out_prologue, w_out, x,
                    tm=tm, tn=512, name="rwkv_out")
    return x_new, _unpair_state(s_new), h3[:, n_valid - 1]


def kernel(x_prompt, x_sample, state_a, cache_b0, cache_b1, cache_b2, state_c_wkv, state_c_shift,
           norm_ffn1, ffn1_in, ffn1_out, norm_mix, norm_ffn2, ffn2_in, ffn2_out, norm_final,
           a_in, a_out, b_in, b_out,
           c_mu, c_rkv, c_w0, c_w1, c_w2, c_a0, c_a1, c_a2, c_g1, c_g2, c_kk, c_ka, c_rk,
           c_ln_w, c_ln_b, c_out):
    bp, seq, d = x_prompt.shape
    bs, n_new, _ = x_sample.shape
    cache_b = (cache_b0, cache_b1, cache_b2)
    tm_p, tm_s = 512, bs * SAMPLE_ROWS
    xp = x_prompt.reshape(bp * seq, d)
    xs = jnp.pad(x_sample, ((0, 0), (0, SAMPLE_ROWS - n_new), (0, 0))).reshape(bs * SAMPLE_ROWS, d)
    bf = lambda w: w.astype(BF16)
    ident = lambda a: a

    a_p, a_d = [], []
    b_p = [[] for _ in B_GROUPS]
    b_d = [[] for _ in B_GROUPS]
    c_wkv_p, c_sh_p, c_wkv_d, c_sh_d = [], [], [], []
    y_prompt = y_sample = None
    for i in range(DEPTH):
        kind, j = i % N_MIXERS, i // N_MIXERS
        w1i, w1o = bf(ffn1_in[i]), bf(ffn1_out[i])
        xp = _ffn(xp, norm_ffn1[i], w1i, w1o, tm=tm_p)
        xs = _ffn(xs, norm_ffn1[i], w1i, w1o, tm=tm_s)
        if kind == 0:
            w_in, w_out = bf(a_in[j]), bf(a_out[j])
            n_chunks = seq // RET_CHUNK
            pp = _norm_mm(xp, norm_mix[i], w_in, tm=tm_p, tn=1024)
            og, s_new = _retention(pp.reshape(bp * n_chunks, RET_CHUNK, -1),
                                   jnp.zeros((bp, A_HEADS, A_DK, A_DV), F32), 0,
                                   n_seq=bp, n_chunks=n_chunks, rows=RET_CHUNK, c_true=RET_CHUNK)
            xp = _mm_res([og.reshape(bp * seq, A_V)], [True], ident, w_out, xp, tm=tm_p, tn=512, name="ret_out")
            a_p.append(s_new)
            ps = _norm_mm(xs, norm_mix[i], w_in, tm=tm_s, tn=1024)
            og, s_new = _retention(ps.reshape(bs, SAMPLE_ROWS, -1), state_a[j], PAST_LEN,
                                   n_seq=bs, n_chunks=1, rows=SAMPLE_ROWS, c_true=n_new)
            xs = _mm_res([og.reshape(bs * SAMPLE_ROWS, A_V)], [True], ident, w_out, xs, tm=tm_s, tn=512, name="ret_out")
            a_d.append(s_new)
        elif kind == 1:
            w_in, w_out = bf(b_in[j]), bf(b_out[j])
            hg, e = B_HEADS_PER_GROUP, B_HEAD_DIM
            pp = _norm_mm(xp, norm_mix[i], w_in, tm=tm_p, tn=1536)
            ps = _norm_mm(xs, norm_mix[i], w_in, tm=tm_s, tn=1536)
            p6 = pp.reshape(bp, seq, 3, N_GROUPS, hg, e)
            s6 = ps.reshape(bs, SAMPLE_ROWS, 3, N_GROUPS, hg, e)
            outs_p, lses_p, outs_s, lses_s = [], [], [], []
            for g, (win, dil) in enumerate(B_GROUPS):
                o, lse = _dilated_prompt(pp, g, dil, n_seq=bp, seq=seq)
                outs_p.append(o)
                lses_p.append(lse)
                keep = min(win, seq)
                b_p[g].append(p6[:, seq - keep:, 1:3, g])
                cache = cache_b[g][j]
                o, lse = _dilated_sample(ps.reshape(bs, SAMPLE_ROWS, -1),
                                         cache.reshape(bs, cache.shape[1], 2 * hg * e), g, dil, n_new=n_new)
                outs_s.append(o.reshape(bs * SAMPLE_ROWS, hg * e))
                lses_s.append(lse.reshape(bs * SAMPLE_ROWS, hg * e))
                b_d[g].append(s6[:, :n_new, 1:3, g])
            xp = _mm_res(outs_p + lses_p, [True] * 6, _merge_prologue, w_out, xp, tm=tm_p, tn=512, name="dil_out")
            xs = _mm_res(outs_s + lses_s, [True] * 6, _merge_prologue, w_out, xs, tm=tm_s, tn=512, name="dil_out")
        else:
            par = (norm_mix[i], c_mu[j], bf(c_rkv[j]), c_w0[j], bf(c_w1[j]), bf(c_w2[j]), c_a0[j],
                   bf(c_a1[j]), bf(c_a2[j]), bf(c_g1[j]), bf(c_g2[j]), c_kk[j], c_ka[j], c_rk[j],
                   c_ln_w[j], c_ln_b[j], bf(c_out[j]))
            xp, s_p, sh_p = _rwkv_mixer(xp, jnp.zeros((bp, d), F32),
                                        jnp.zeros((bp, C_HEADS, C_HEAD, C_HEAD), F32), par,
                                        n_seq=bp, rows_per_seq=seq, n_valid=seq, tm=tm_p)
            xs, s_d, sh_d = _rwkv_mixer(xs, state_c_shift[j], state_c_wkv[j], par,
                                        n_seq=bs, rows_per_seq=SAMPLE_ROWS, n_valid=n_new, tm=tm_s)
            c_wkv_p.append(s_p)
            c_sh_p.append(sh_p)
            c_wkv_d.append(s_d)
            c_sh_d.append(sh_d)
        w2i, w2o = bf(ffn2_in[i]), bf(ffn2_out[i])
        if i == DEPTH - 1:
            xp, y_prompt = _ffn(xp, norm_ffn2[i], w2i, w2o, tm=tm_p, final_gain=norm_final)
            xs, y_sample = _ffn(xs, norm_ffn2[i], w2i, w2o, tm=tm_s, final_gain=norm_final)
        else:
            xp = _ffn(xp, norm_ffn2[i], w2i, w2o, tm=tm_p)
            xs = _ffn(xs, norm_ffn2[i], w2i, w2o, tm=tm_s)
    y_prompt = y_prompt.reshape(bp, seq, d)
    y_sample = y_sample.reshape(bs, SAMPLE_ROWS, d)[:, :n_new]
    return (y_prompt, y_sample,
            jnp.stack(a_p), jnp.stack(a_d),
            jnp.stack(b_p[0]), jnp.stack(b_p[1]), jnp.stack(b_p[2]),
            jnp.stack(b_d[0]), jnp.stack(b_d[1]), jnp.stack(b_d[2]),
            jnp.stack(c_wkv_p), jnp.stack(c_sh_p), jnp.stack(c_wkv_d), jnp.stack(c_sh_d))
```
